```python
import jax, jax.numpy as jnp
from jax import lax
import numpy as np

D_MODEL = 2048
BATCH = 8
SEQ = 4096
DEPTH = 1

CHUNK = 64
MIX_WIDTH = D_MODEL
GMLP_WIDTH = MIX_WIDTH // 2
GMLP_GROUPS = 8
GMLP_GROUP_DIM = GMLP_WIDTH // GMLP_GROUPS
GMLP_BLOCK = 128
MLA_WIDTH = MIX_WIDTH - GMLP_WIDTH
MLA_HEADS = 8
QK_NOPE_DIM = 128
QK_ROPE_DIM = 64
V_HEAD_DIM = MLA_WIDTH // MLA_HEADS
Q_LORA_RANK = 512
KV_LORA_RANK = 256
QK_HEAD_DIM = QK_NOPE_DIM + QK_ROPE_DIM
D_FF = 4 * D_MODEL
ROPE_THETA = 10000.0
EPS = 1e-6
Q_BLOCK = 128
N_MOD = 6
IN_SPLITS = (GMLP_WIDTH, GMLP_WIDTH, Q_LORA_RANK, KV_LORA_RANK, QK_ROPE_DIM)
IN_PROJ = sum(IN_SPLITS)

kernel_name = "hybrid_gmlp_mla_sandwich_adaln_block"


def rms_norm(x, g):
    xf = x.astype(jnp.float32)
    y = xf * lax.rsqrt(jnp.mean(xf * xf, axis=-1, keepdims=True) + EPS)
    return (y * g.astype(jnp.float32)).astype(x.dtype)


def layer_norm(x, g, b):
    xf = x.astype(jnp.float32)
    mu = jnp.mean(xf, axis=-1, keepdims=True)
    var = jnp.mean(jnp.square(xf - mu), axis=-1, keepdims=True)
    y = (xf - mu) * lax.rsqrt(var + EPS)
    return (y * g.astype(jnp.float32) + b.astype(jnp.float32)).astype(x.dtype)


def rope(x, positions):
    dr = x.shape[-1]
    freqs = ROPE_THETA ** (-jnp.arange(0, dr, 2, dtype=jnp.float32) / dr)
    ang = positions.astype(jnp.float32)[:, :, None] * freqs
    cos = jnp.cos(ang)[:, :, None, :].astype(x.dtype)
    sin = jnp.sin(ang)[:, :, None, :].astype(x.dtype)
    x1, x2 = jnp.split(x, 2, axis=-1)
    return jnp.concatenate([x1 * cos - x2 * sin, x2 * cos + x1 * sin], axis=-1)


def gmlp_mixer(u, v, ln_g, ln_b, w_s, b_s):
    u = jax.nn.gelu(u, approximate=False)
    v = layer_norm(jax.nn.gelu(v, approximate=False), ln_g, ln_b)
    B, S, _ = v.shape
    nb = S // GMLP_BLOCK
    vb = v.reshape(B, nb, GMLP_BLOCK, GMLP_GROUPS, GMLP_GROUP_DIM)
    ch = jnp.arange(GMLP_BLOCK) // CHUNK
    mask = (ch[None, :] <= ch[:, None]).astype(w_s.dtype)
    w = w_s * mask[None]
    s = jnp.einsum('gij,bnjgc->bnigc', w, vb) + jnp.transpose(b_s)[None, None, :, :, None]
    return u * s.reshape(B, S, GMLP_WIDTH)


def mla_mixer(q_lat, kv_lat, k_rope, positions, q_norm_g, w_q_b, kv_norm_g, w_kv_b):
    B, S, _ = q_lat.shape
    q = (rms_norm(q_lat, q_norm_g) @ w_q_b).reshape(B, S, MLA_HEADS, QK_HEAD_DIM)
    q_nope, q_pe = q[..., :QK_NOPE_DIM], q[..., QK_NOPE_DIM:]
    q = jnp.concatenate([q_nope, rope(q_pe, positions)], axis=-1)
    kv = (rms_norm(kv_lat, kv_norm_g) @ w_kv_b).reshape(B, S, MLA_HEADS, QK_NOPE_DIM + V_HEAD_DIM)
    k_nope, v = kv[..., :QK_NOPE_DIM], kv[..., QK_NOPE_DIM:]
    k_pe = rope(k_rope[:, :, None, :], positions)
    k = jnp.concatenate([k_nope, jnp.broadcast_to(k_pe, (B, S, MLA_HEADS, QK_ROPE_DIM))], axis=-1)
    scale = QK_HEAD_DIM ** -0.5
    outs = []
    for blk in range(S // Q_BLOCK):
        q0 = blk * Q_BLOCK
        kend = q0 + Q_BLOCK
        sc = jnp.einsum('bqhd,bkhd->bhqk', q[:, q0:kend], k[:, :kend]).astype(jnp.float32) * scale
        qc = (q0 + jnp.arange(Q_BLOCK)) // CHUNK
        kc = jnp.arange(kend) // CHUNK
        sc = jnp.where(kc[None, :] <= qc[:, None], sc, -jnp.inf)
        p = jax.nn.softmax(sc, axis=-1).astype(v.dtype)
        outs.append(jnp.einsum('bhqk,bkhd->bqhd', p, v[:, :kend]))
    return jnp.concatenate(outs, axis=1).reshape(B, S, MLA_WIDTH)


def setup_inputs(seed: int = 0) -> dict:
    key = jax.random.key(seed)
    ks = jax.random.split(key, 24)
    f32 = jnp.float32
    L = DEPTH

    def nrm(k, shape, s):
        return jax.random.normal(k, shape, f32) * s

    def gain(k, n):
        return 1.0 + 0.05 * jax.random.normal(k, (L, n), f32)

    return {
        "x": jax.random.normal(ks[0], (BATCH, SEQ, D_MODEL), f32),
        "c": jax.random.normal(ks[1], (BATCH, D_MODEL), f32),
        "positions": jnp.broadcast_to(jnp.arange(SEQ, dtype=jnp.int32)[None, :], (BATCH, SEQ)),
        "w_ada": nrm(ks[2], (L, D_MODEL, N_MOD * D_MODEL), 0.5 * D_MODEL ** -0.5),
        "b_ada": nrm(ks[3], (L, N_MOD * D_MODEL), 0.02),
        "pre_norm1_g": gain(ks[4], D_MODEL),
        "w_in": nrm(ks[5], (L, D_MODEL, IN_PROJ), D_MODEL ** -0.5),
        "ln_v_g": gain(ks[6], GMLP_WIDTH),
        "ln_v_b": nrm(ks[7], (L, GMLP_WIDTH), 0.02),
        "w_s": nrm(ks[8], (L, GMLP_GROUPS, GMLP_BLOCK, GMLP_BLOCK), 0.5 * GMLP_BLOCK ** -0.5),
        "b_s": 1.0 + 0.05 * jax.random.normal(ks[9], (L, GMLP_GROUPS, GMLP_BLOCK), f32),
        "q_norm_g": gain(ks[10], Q_LORA_RANK),
        "w_q_b": nrm(ks[11], (L, Q_LORA_RANK, MLA_HEADS * QK_HEAD_DIM), Q_LORA_RANK ** -0.5),
        "kv_norm_g": gain(ks[12], KV_LORA_RANK),
        "w_kv_b": nrm(ks[13], (L, KV_LORA_RANK, MLA_HEADS * (QK_NOPE_DIM + V_HEAD_DIM)), KV_LORA_RANK ** -0.5),
        "grp_norm_a_g": gain(ks[14], GMLP_WIDTH),
        "grp_norm_b_g": gain(ks[15], MLA_WIDTH),
        "w_o": nrm(ks[16], (L, MIX_WIDTH, D_MODEL), MIX_WIDTH ** -0.5),
        "post_norm1_g": gain(ks[17], D_MODEL),
        "pre_norm2_g": gain(ks[18], D_MODEL),
        "w_ff1": nrm(ks[19], (L, D_MODEL, D_FF), D_MODEL ** -0.5),
        "w_ff2": nrm(ks[20], (L, D_FF, D_MODEL), D_FF ** -0.5),
        "post_norm2_g": gain(ks[21], D_MODEL),
    }


def reference(x, c, positions, w_ada, b_ada, pre_norm1_g, w_in, ln_v_g, ln_v_b, w_s, b_s,
              q_norm_g, w_q_b, kv_norm_g, w_kv_b, grp_norm_a_g, grp_norm_b_g, w_o,
              post_norm1_g, pre_norm2_g, w_ff1, w_ff2, post_norm2_g):
    cs = jax.nn.silu(c)
    offs = np.cumsum(IN_SPLITS)[:-1].tolist()
    for l in range(DEPTH):
        mod = cs @ w_ada[l] + b_ada[l]
        sh1, sc1, g1, sh2, sc2, g2 = [m[:, None, :] for m in jnp.split(mod, N_MOD, axis=-1)]

        h = rms_norm(x, pre_norm1_g[l]) * (1.0 + sc1) + sh1
        z = h @ w_in[l]
        u, v, q_lat, kv_lat, k_rope = jnp.split(z, offs, axis=-1)
        a = gmlp_mixer(u, v, ln_v_g[l], ln_v_b[l], w_s[l], b_s[l])
        m = mla_mixer(q_lat, kv_lat, k_rope, positions, q_norm_g[l], w_q_b[l],
                      kv_norm_g[l], w_kv_b[l])
        y = jnp.concatenate([rms_norm(a, grp_norm_a_g[l]), rms_norm(m, grp_norm_b_g[l])], axis=-1) @ w_o[l]
        x = x + g1 * rms_norm(y, post_norm1_g[l])

        h = rms_norm(x, pre_norm2_g[l]) * (1.0 + sc2) + sh2
        f = jnp.square(jax.nn.relu(h @ w_ff1[l])) @ w_ff2[l]
        x = x + g2 * rms_norm(f, post_norm2_g[l])
    return x
```

```python
import functools
import math

import jax
import jax.numpy as jnp
from jax import lax
from jax.experimental import pallas as pl
from jax.experimental.pallas import tpu as pltpu

D_MODEL = 2048
CHUNK = 64
GMLP_WIDTH = 1024
GMLP_GROUPS = 8
GMLP_GROUP_DIM = 128
GMLP_BLOCK = 128
MLA_WIDTH = 1024
MLA_HEADS = 8
QK_NOPE_DIM = 128
QK_ROPE_DIM = 64
V_HEAD_DIM = 128
Q_LORA_RANK = 512
KV_LORA_RANK = 256
QK_HEAD_DIM = QK_NOPE_DIM + QK_ROPE_DIM
D_FF = 4 * D_MODEL
ROPE_THETA = 10000.0
EPS = 1e-6
N_MOD = 6

LANES = 128
QK_PAD_DIM = 2 * LANES
ROPE_HALF = QK_ROPE_DIM // 2
IN_LAT = GMLP_WIDTH * 2
IN_PROJ_PAD = IN_LAT + Q_LORA_RANK + KV_LORA_RANK + LANES

VMEM_LIMIT = 56 * 1024 * 1024

ADA_TN = 1024
MIX_TM = 512
ATT_TQ = 512
OUT_TM = 512
FFN_TM = 1024
FFN_TF = 512

_SQRT_HALF = math.sqrt(0.5)


def _rms(x, g):
    return x * lax.rsqrt(jnp.mean(x * x, axis=-1, keepdims=True) + EPS) * g


def _gelu(x):
    return 0.5 * x * (1.0 + lax.erf(x * _SQRT_HALF))


def _bf16_dot(a, b):
    return jnp.dot(a, b, preferred_element_type=jnp.float32)


def _const_spec(shape):
    nd = len(shape)
    return pl.BlockSpec(shape, lambda *_: (0,) * nd, pipeline_mode=pl.Buffered(1))


def _adaln_kernel(c_ref, w_ref, b_ref, o_ref):
    c = c_ref[...]
    cs = (c * jax.nn.sigmoid(c)).astype(jnp.bfloat16)
    o_ref[...] = _bf16_dot(cs, w_ref[...].astype(jnp.bfloat16)) + b_ref[...]


def _adaln(c, w_ada, b_ada):
    b, d = c.shape
    n = w_ada.shape[1]
    return pl.pallas_call(
        _adaln_kernel,
        grid=(n // ADA_TN,),
        in_specs=[
            pl.BlockSpec((b, d), lambda j: (0, 0)),
            pl.BlockSpec((d, ADA_TN), lambda j: (0, j)),
            pl.BlockSpec((1, ADA_TN), lambda j: (0, j)),
        ],
        out_specs=pl.BlockSpec((b, ADA_TN), lambda j: (0, j)),
        out_shape=jax.ShapeDtypeStruct((b, n), jnp.float32),
        compiler_params=pltpu.CompilerParams(
            dimension_semantics=("parallel",), vmem_limit_bytes=VMEM_LIMIT),
        name="adaln",
    )(c, w_ada, b_ada.reshape(1, n))


def _mixer_in_kernel(x_ref, pos_ref, mod_ref, g1_ref, w_in_ref, lng_ref, lnb_ref,
                     ws_ref, bs_ref, qg_ref, kvg_ref, wq_ref, wkv_ref, ga_ref, fr_ref,
                     an_ref, q_ref, k_ref, v_ref, a_scr):
    tm = x_ref.shape[0]
    sh1 = mod_ref[0, 0:1, :]
    sc1 = mod_ref[0, 1:2, :]
    h = (_rms(x_ref[...], g1_ref[...]) * (1.0 + sc1) + sh1).astype(jnp.bfloat16)

    ug = _gelu(_bf16_dot(h, w_in_ref[:, 0:GMLP_WIDTH]))
    vg = _gelu(_bf16_dot(h, w_in_ref[:, GMLP_WIDTH:IN_LAT]))
    mu = jnp.mean(vg, axis=-1, keepdims=True)
    vc = vg - mu
    var = jnp.mean(vc * vc, axis=-1, keepdims=True)
    vn = (vc * lax.rsqrt(var + EPS) * lng_ref[...] + lnb_ref[...]).astype(jnp.bfloat16)

    row_chunk = lax.broadcasted_iota(jnp.int32, (GMLP_BLOCK, GMLP_BLOCK), 0) // CHUNK
    col_chunk = lax.broadcasted_iota(jnp.int32, (GMLP_BLOCK, GMLP_BLOCK), 1) // CHUNK
    visible = col_chunk <= row_chunk
    n_blk = tm // GMLP_BLOCK
    for g in range(GMLP_GROUPS):
        c0 = g * GMLP_GROUP_DIM
        w_g = jnp.where(visible, ws_ref[g], 0.0).astype(jnp.bfloat16)
        v_g = jnp.concatenate(
            [vn[r * GMLP_BLOCK:(r + 1) * GMLP_BLOCK, c0:c0 + GMLP_GROUP_DIM] for r in range(n_blk)],
            axis=1)
        s_g = _bf16_dot(w_g, v_g)
        bias = bs_ref[:, g:g + 1]
        for r in range(n_blk):
            rows = slice(r * GMLP_BLOCK, (r + 1) * GMLP_BLOCK)
            s_rg = s_g[:, r * GMLP_GROUP_DIM:(r + 1) * GMLP_GROUP_DIM] + bias
            a_scr[rows, c0:c0 + GMLP_GROUP_DIM] = ug[rows, c0:c0 + GMLP_GROUP_DIM] * s_rg
    an_ref[...] = _rms(a_scr[...], ga_ref[...]).astype(jnp.bfloat16)

    o_q = IN_LAT
    o_kv = o_q + Q_LORA_RANK
    o_kr = o_kv + KV_LORA_RANK
    qn = _rms(_bf16_dot(h, w_in_ref[:, o_q:o_kv]), qg_ref[...]).astype(jnp.bfloat16)
    kvn = _rms(_bf16_dot(h, w_in_ref[:, o_kv:o_kr]), kvg_ref[...]).astype(jnp.bfloat16)
    k_pe = _bf16_dot(h, w_in_ref[:, o_kr:o_kr + LANES])

    ang = pos_ref[...].astype(jnp.float32) * fr_ref[...]
    lane = lax.broadcasted_iota(jnp.int32, (tm, LANES), 1)
    in_rope = lane < QK_ROPE_DIM
    cos_t = jnp.where(in_rope, jnp.cos(ang), 0.0)
    sin_t = jnp.where(in_rope, jnp.sin(ang), 0.0)

    def rope(xcol):
        return xcol * cos_t + pltpu.roll(xcol, 2 * ROPE_HALF, 1) * sin_t

    k_pe_rot = rope(k_pe).astype(jnp.bfloat16)
    q_all = _bf16_dot(qn, wq_ref[...])
    kv_all = _bf16_dot(kvn, wkv_ref[...])
    scale = QK_HEAD_DIM ** -0.5
    for hd in range(MLA_HEADS):
        b0 = hd * QK_PAD_DIM
        q_ref[:, b0:b0 + LANES] = (q_all[:, b0:b0 + LANES] * scale).astype(jnp.bfloat16)
        q_ref[:, b0 + LANES:b0 + QK_PAD_DIM] = (
            rope(q_all[:, b0 + LANES:b0 + QK_PAD_DIM]) * scale).astype(jnp.bfloat16)
        k_ref[:, b0:b0 + LANES] = kv_all[:, hd * LANES:(hd + 1) * LANES].astype(jnp.bfloat16)
        k_ref[:, b0 + LANES:b0 + QK_PAD_DIM] = k_pe_rot
    v_ref[...] = kv_all[:, MLA_WIDTH:].astype(jnp.bfloat16)


def _mixer_in(x2, pos2, mod3, g1, w_in_p, ln_g, ln_b, w_s, bs_t, q_g, kv_g, w_q_p, w_kv_p,
              ga, freqs, seq):
    m, d = x2.shape
    tm = MIX_TM
    per_batch = seq // tm
    row = lambda i: (i, 0)
    out_shapes = (
        jax.ShapeDtypeStruct((m, GMLP_WIDTH), jnp.bfloat16),
        jax.ShapeDtypeStruct((m, MLA_HEADS * QK_PAD_DIM), jnp.bfloat16),
        jax.ShapeDtypeStruct((m, MLA_HEADS * QK_PAD_DIM), jnp.bfloat16),
        jax.ShapeDtypeStruct((m, MLA_WIDTH), jnp.bfloat16),
    )
    return pl.pallas_call(
        _mixer_in_kernel,
        grid=(m // tm,),
        in_specs=[
            pl.BlockSpec((tm, d), row),
            pl.BlockSpec((tm, 1), row),
            pl.BlockSpec((1, N_MOD, d), lambda i: (i // per_batch, 0, 0)),
            _const_spec(g1.shape),
            _const_spec(w_in_p.shape),
            _const_spec(ln_g.shape),
            _const_spec(ln_b.shape),
            _const_spec(w_s.shape),
            _const_spec(bs_t.shape),
            _const_spec(q_g.shape),
            _const_spec(kv_g.shape),
            _const_spec(w_q_p.shape),
            _const_spec(w_kv_p.shape),
            _const_spec(ga.shape),
            _const_spec(freqs.shape),
        ],
        out_specs=[
            pl.BlockSpec((tm, GMLP_WIDTH), row),
            pl.BlockSpec((tm, MLA_HEADS * QK_PAD_DIM), row),
            pl.BlockSpec((tm, MLA_HEADS * QK_PAD_DIM), row),
            pl.BlockSpec((tm, MLA_WIDTH), row),
        ],
        out_shape=out_shapes,
        scratch_shapes=[pltpu.VMEM((tm, GMLP_WIDTH), jnp.float32)],
        compiler_params=pltpu.CompilerParams(
            dimension_semantics=("parallel",), vmem_limit_bytes=VMEM_LIMIT),
        name="mixer_in",
    )(x2, pos2, mod3, g1, w_in_p, ln_g, ln_b, w_s, bs_t, q_g, kv_g, w_q_p, w_kv_p, ga, freqs)


_NT_DIMS = (((1,), (1,)), ((), ()))


def _attention_kernel(q_ref, k_ref, v_ref, o_ref, m_scr, l_scr, acc_scr):
    seq = q_ref.shape[0]
    tq = ATT_TQ
    row_chunk = lax.broadcasted_iota(jnp.int32, (tq, tq), 0) // CHUNK
    col_chunk = lax.broadcasted_iota(jnp.int32, (tq, tq), 1) // CHUNK
    visible = col_chunk <= row_chunk

    def q_body(qi, carry):
        q0 = pl.multiple_of(qi * tq, tq)
        q = q_ref[pl.ds(q0, tq), :]

        s = lax.dot_general(q, k_ref[pl.ds(q0, tq), :], _NT_DIMS,
                            preferred_element_type=jnp.float32)
        s = jnp.where(visible, s, -jnp.inf)
        m0 = jnp.max(s, axis=-1, keepdims=True)
        p = jnp.exp(s - m0)
        m_scr[...] = m0
        l_scr[...] = jnp.sum(p, axis=-1, keepdims=True)
        acc_scr[...] = _bf16_dot(p.astype(jnp.bfloat16), v_ref[pl.ds(q0, tq), :])

        def kv_body(j, c):
            k0 = pl.multiple_of(j * tq, tq)
            s = lax.dot_general(q, k_ref[pl.ds(k0, tq), :], _NT_DIMS,
                                preferred_element_type=jnp.float32)
            m_prev = m_scr[...]
            m_new = jnp.maximum(m_prev, jnp.max(s, axis=-1, keepdims=True))
            alpha = jnp.exp(m_prev - m_new)
            p = jnp.exp(s - m_new)
            l_scr[...] = alpha * l_scr[...] + jnp.sum(p, axis=-1, keepdims=True)
            acc_scr[...] = alpha * acc_scr[...] + _bf16_dot(
                p.astype(jnp.bfloat16), v_ref[pl.ds(k0, tq), :])
            m_scr[...] = m_new
            return c

        lax.fori_loop(0, qi, kv_body, 0)
        o_ref[pl.ds(q0, tq), :] = (acc_scr[...] / l_scr[...]).astype(o_ref.dtype)
        return carry

    lax.fori_loop(0, seq // tq, q_body, 0)


def _attention(q, k, v, batch, seq):
    m = q.shape[0]
    return pl.pallas_call(
        _attention_kernel,
        grid=(batch, MLA_HEADS),
        in_specs=[
            pl.BlockSpec((seq, QK_PAD_DIM), lambda b, h: (b, h)),
            pl.BlockSpec((seq, QK_PAD_DIM), lambda b, h: (b, h)),
            pl.BlockSpec((seq, V_HEAD_DIM), lambda b, h: (b, h)),
        ],
        out_specs=pl.BlockSpec((seq, V_HEAD_DIM), lambda b, h: (b, h)),
        out_shape=jax.ShapeDtypeStruct((m, MLA_WIDTH), jnp.bfloat16),
        scratch_shapes=[
            pltpu.VMEM((ATT_TQ, 1), jnp.float32),
            pltpu.VMEM((ATT_TQ, 1), jnp.float32),
            pltpu.VMEM((ATT_TQ, V_HEAD_DIM), jnp.float32),
        ],
        compiler_params=pltpu.CompilerParams(
            dimension_semantics=("parallel", "parallel"), vmem_limit_bytes=VMEM_LIMIT),
        name="attention",
    )(q, k, v)


def _mixer_out_kernel(x_ref, an_ref, m_ref, mod_ref, gb_ref, wo_ref, gp1_ref, g2_ref,
                      x1_ref, h2_ref):
    g1 = mod_ref[0, 2:3, :]
    sh2 = mod_ref[0, 3:4, :]
    sc2 = mod_ref[0, 4:5, :]
    mn = _rms(m_ref[...].astype(jnp.float32), gb_ref[...]).astype(jnp.bfloat16)
    y = _bf16_dot(an_ref[...], wo_ref[0:GMLP_WIDTH, :]) + _bf16_dot(mn, wo_ref[GMLP_WIDTH:, :])
    x1 = x_ref[...] + g1 * _rms(y, gp1_ref[...])
    x1_ref[...] = x1
    h2_ref[...] = (_rms(x1, g2_ref[...]) * (1.0 + sc2) + sh2).astype(jnp.bfloat16)


def _mixer_out(x2, an, mo, mod3, gb, w_o, gp1, g2, seq):
    m, d = x2.shape
    tm = OUT_TM
    per_batch = seq // tm
    row = lambda i: (i, 0)
    return pl.pallas_call(
        _mixer_out_kernel,
        grid=(m // tm,),
        in_specs=[
            pl.BlockSpec((tm, d), row),
            pl.BlockSpec((tm, GMLP_WIDTH), row),
            pl.BlockSpec((tm, MLA_WIDTH), row),
            pl.BlockSpec((1, N_MOD, d), lambda i: (i // per_batch, 0, 0)),
            _const_spec(gb.shape),
            _const_spec(w_o.shape),
            _const_spec(gp1.shape),
            _const_spec(g2.shape),
        ],
        out_specs=[pl.BlockSpec((tm, d), row), pl.BlockSpec((tm, d), row)],
        out_shape=(jax.ShapeDtypeStruct((m, d), jnp.float32),
                   jax.ShapeDtypeStruct((m, d), jnp.bfloat16)),
        compiler_params=pltpu.CompilerParams(
            dimension_semantics=("parallel",), vmem_limit_bytes=VMEM_LIMIT),
        name="mixer_out",
    )(x2, an, mo, mod3, gb, w_o, gp1, g2)


def _ffn_kernel(h_ref, x1_ref, mod_ref, w1_ref, w2_ref, gp2_ref, o_ref):
    j = pl.program_id(1)
    t = jnp.maximum(_bf16_dot(h_ref[...], w1_ref[...]), 0.0)
    part = _bf16_dot((t * t).astype(jnp.bfloat16), w2_ref[...])

    @pl.when(j == 0)
    def _():
        o_ref[...] = part

    @pl.when(j > 0)
    def _():
        o_ref[...] += part

    @pl.when(j == pl.num_programs(1) - 1)
    def _():
        g2 = mod_ref[0, 5:6, :]
        o_ref[...] = x1_ref[...] + g2 * _rms(o_ref[...], gp2_ref[...])


def _ffn(h2, x1, mod3, w1, w2, gp2, seq):
    m, d = x1.shape
    tm, tf = FFN_TM, FFN_TF
    per_batch = seq // tm
    return pl.pallas_call(
        _ffn_kernel,
        grid=(m // tm, D_FF // tf),
        in_specs=[
            pl.BlockSpec((tm, d), lambda i, j: (i, 0)),
            pl.BlockSpec((tm, d), lambda i, j: (i, 0), pipeline_mode=pl.Buffered(1)),
            pl.BlockSpec((1, N_MOD, d), lambda i, j: (i // per_batch, 0, 0)),
            pl.BlockSpec((d, tf), lambda i, j: (0, j)),
            pl.BlockSpec((tf, d), lambda i, j: (j, 0)),
            _const_spec(gp2.shape),
        ],
        out_specs=pl.BlockSpec((tm, d), lambda i, j: (i, 0)),
        out_shape=jax.ShapeDtypeStruct((m, d), jnp.float32),
        compiler_params=pltpu.CompilerParams(
            dimension_semantics=("parallel", "arbitrary"), vmem_limit_bytes=VMEM_LIMIT),
        name="ffn",
    )(h2, x1, mod3, w1, w2, gp2)


def _rope_cols(w_pe):
    x1, x2 = w_pe[..., :ROPE_HALF], w_pe[..., ROPE_HALF:]
    return jnp.concatenate([x1, x2, -x2, x1], axis=-1)


def kernel(x, c, positions, w_ada, b_ada, pre_norm1_g, w_in, ln_v_g, ln_v_b, w_s, b_s, q_norm_g, w_q_b, kv_norm_g, w_kv_b, grp_norm_a_g, grp_norm_b_g, w_o, post_norm1_g, pre_norm2_g, w_ff1, w_ff2, post_norm2_g):
    batch, seq, d = x.shape
    depth = w_ada.shape[0]
    bf = jnp.bfloat16
    freqs = ROPE_THETA ** (-jnp.arange(0, QK_ROPE_DIM, 2, dtype=jnp.float32) / QK_ROPE_DIM)
    freqs = jnp.tile(freqs, LANES // ROPE_HALF).reshape(1, LANES)
    pos2 = positions.reshape(batch * seq, 1)
    x2 = x.reshape(batch * seq, d)
    o_kr = IN_LAT + Q_LORA_RANK + KV_LORA_RANK

    for l in range(depth):
        w_in_p = jnp.concatenate([w_in[l][:, :o_kr], _rope_cols(w_in[l][:, o_kr:])], axis=1).astype(bf)
        wq = w_q_b[l].reshape(Q_LORA_RANK, MLA_HEADS, QK_HEAD_DIM)
        w_q_p = jnp.concatenate([wq[..., :QK_NOPE_DIM], _rope_cols(wq[..., QK_NOPE_DIM:])], axis=-1)
        w_q_p = w_q_p.reshape(Q_LORA_RANK, MLA_HEADS * QK_PAD_DIM).astype(bf)
        wkv = w_kv_b[l].reshape(KV_LORA_RANK, MLA_HEADS, QK_NOPE_DIM + V_HEAD_DIM)
        w_kv_p = jnp.concatenate(
            [wkv[..., :QK_NOPE_DIM].reshape(KV_LORA_RANK, -1),
             wkv[..., QK_NOPE_DIM:].reshape(KV_LORA_RANK, -1)], axis=1).astype(bf)

        mod3 = _adaln(c, w_ada[l], b_ada[l]).reshape(batch, N_MOD, d)
        an, q, k, v = _mixer_in(
            x2, pos2, mod3, pre_norm1_g[l].reshape(1, d), w_in_p,
            ln_v_g[l].reshape(1, -1), ln_v_b[l].reshape(1, -1), w_s[l], b_s[l].T,
            q_norm_g[l].reshape(1, -1), kv_norm_g[l].reshape(1, -1), w_q_p, w_kv_p,
            grp_norm_a_g[l].reshape(1, -1), freqs, seq)
        mo = _attention(q, k, v, batch, seq)
        x1, h2 = _mixer_out(
            x2, an, mo, mod3, grp_norm_b_g[l].reshape(1, -1), w_o[l].astype(bf),
            post_norm1_g[l].reshape(1, d), pre_norm2_g[l].reshape(1, d), seq)
        x2 = _ffn(h2, x1, mod3, w_ff1[l].astype(bf), w_ff2[l].astype(bf),
                  post_norm2_g[l].reshape(1, d), seq)
    return x2.reshape(batch, seq, d)
```

```python
import functools
import math

import jax
import jax.numpy as jnp
from jax import lax
from jax.experimental import pallas as pl
from jax.experimental.pallas import tpu as pltpu

D_MODEL = 2048
CHUNK = 64
GMLP_WIDTH = 1024
GMLP_GROUPS = 8
GMLP_GROUP_DIM = 128
GMLP_BLOCK = 128
MLA_WIDTH = 1024
MLA_HEADS = 8
QK_NOPE_DIM = 128
QK_ROPE_DIM = 64
V_HEAD_DIM = 128
Q_LORA_RANK = 512
KV_LORA_RANK = 256
QK_HEAD_DIM = QK_NOPE_DIM + QK_ROPE_DIM
D_FF = 4 * D_MODEL
ROPE_THETA = 10000.0
EPS = 1e-6
N_MOD = 6

LANES = 128
QK_PAD_DIM = 2 * LANES
ROPE_HALF = QK_ROPE_DIM // 2
IN_LAT = GMLP_WIDTH * 2
IN_PROJ_PAD = IN_LAT + Q_LORA_RANK + KV_LORA_RANK + LANES

VMEM_LIMIT = 56 * 1024 * 1024

ADA_TN = 1024
MIX_TM = 512
ATT_TQ = 1024
ATT_TK = 256
ATT_ROWS = 128
ATT_PV_ROWS = 512
OUT_TM = 512
FFN_TM = 1024
FFN_TF = 512

_SQRT_HALF = math.sqrt(0.5)
_LOG2_E = math.log2(math.e)


def _rms(x, g):
    return x * lax.rsqrt(jnp.mean(x * x, axis=-1, keepdims=True) + EPS) * g


def _gelu(x):
    return 0.5 * x * (1.0 + lax.erf(x * _SQRT_HALF))


def _bf16_dot(a, b):
    return jnp.dot(a, b, preferred_element_type=jnp.float32)


def _const_spec(shape):
    nd = len(shape)
    return pl.BlockSpec(shape, lambda *_: (0,) * nd, pipeline_mode=pl.Buffered(1))


def _adaln_kernel(c_ref, w_ref, b_ref, o_ref):
    c = c_ref[...]
    cs = (c * jax.nn.sigmoid(c)).astype(jnp.bfloat16)
    o_ref[...] = _bf16_dot(cs, w_ref[...].astype(jnp.bfloat16)) + b_ref[...]


def _adaln(c, w_ada, b_ada):
    b, d = c.shape
    n = w_ada.shape[1]
    return pl.pallas_call(
        _adaln_kernel,
        grid=(n // ADA_TN,),
        in_specs=[
            pl.BlockSpec((b, d), lambda j: (0, 0)),
            pl.BlockSpec((d, ADA_TN), lambda j: (0, j)),
            pl.BlockSpec((1, ADA_TN), lambda j: (0, j)),
        ],
        out_specs=pl.BlockSpec((b, ADA_TN), lambda j: (0, j)),
        out_shape=jax.ShapeDtypeStruct((b, n), jnp.float32),
        compiler_params=pltpu.CompilerParams(
            dimension_semantics=("parallel",), vmem_limit_bytes=VMEM_LIMIT),
        name="adaln",
    )(c, w_ada, b_ada.reshape(1, n))


def _mixer_in_kernel(x_ref, pos_ref, mod_ref, g1_ref, w_in_ref, lng_ref, lnb_ref,
                     ws_ref, bs_ref, qg_ref, kvg_ref, wq_ref, wkv_ref, ga_ref, fr_ref,
                     an_ref, q_ref, k_ref, v_ref, a_scr):
    tm = x_ref.shape[0]
    sh1 = mod_ref[0, 0:1, :]
    sc1 = mod_ref[0, 1:2, :]
    h = (_rms(x_ref[...], g1_ref[...]) * (1.0 + sc1) + sh1).astype(jnp.bfloat16)

    ug = _gelu(_bf16_dot(h, w_in_ref[:, 0:GMLP_WIDTH]))
    vg = _gelu(_bf16_dot(h, w_in_ref[:, GMLP_WIDTH:IN_LAT]))
    mu = jnp.mean(vg, axis=-1, keepdims=True)
    vc = vg - mu
    var = jnp.mean(vc * vc, axis=-1, keepdims=True)
    vn = (vc * lax.rsqrt(var + EPS) * lng_ref[...] + lnb_ref[...]).astype(jnp.bfloat16)

    row_chunk = lax.broadcasted_iota(jnp.int32, (GMLP_BLOCK, GMLP_BLOCK), 0) // CHUNK
    col_chunk = lax.broadcasted_iota(jnp.int32, (GMLP_BLOCK, GMLP_BLOCK), 1) // CHUNK
    visible = col_chunk <= row_chunk
    n_blk = tm // GMLP_BLOCK
    for g in range(GMLP_GROUPS):
        c0 = g * GMLP_GROUP_DIM
        w_g = jnp.where(visible, ws_ref[g], 0.0).astype(jnp.bfloat16)
        v_g = jnp.concatenate(
            [vn[r * GMLP_BLOCK:(r + 1) * GMLP_BLOCK, c0:c0 + GMLP_GROUP_DIM] for r in range(n_blk)],
            axis=1)
        s_g = _bf16_dot(w_g, v_g)
        bias = bs_ref[:, g:g + 1]
        for r in range(n_blk):
            rows = slice(r * GMLP_BLOCK, (r + 1) * GMLP_BLOCK)
            s_rg = s_g[:, r * GMLP_GROUP_DIM:(r + 1) * GMLP_GROUP_DIM] + bias
            a_scr[rows, c0:c0 + GMLP_GROUP_DIM] = ug[rows, c0:c0 + GMLP_GROUP_DIM] * s_rg
    an_ref[...] = _rms(a_scr[...], ga_ref[...]).astype(jnp.bfloat16)

    o_q = IN_LAT
    o_kv = o_q + Q_LORA_RANK
    o_kr = o_kv + KV_LORA_RANK
    qn = _rms(_bf16_dot(h, w_in_ref[:, o_q:o_kv]), qg_ref[...]).astype(jnp.bfloat16)
    kvn = _rms(_bf16_dot(h, w_in_ref[:, o_kv:o_kr]), kvg_ref[...]).astype(jnp.bfloat16)
    k_pe = _bf16_dot(h, w_in_ref[:, o_kr:o_kr + LANES])

    ang = pos_ref[...].astype(jnp.float32) * fr_ref[...]
    lane = lax.broadcasted_iota(jnp.int32, (tm, LANES), 1)
    in_rope = lane < QK_ROPE_DIM
    cos_t = jnp.where(in_rope, jnp.cos(ang), 0.0)
    sin_t = jnp.where(in_rope, jnp.sin(ang), 0.0)

    def rope(xcol):
        return xcol * cos_t + pltpu.roll(xcol, 2 * ROPE_HALF, 1) * sin_t

    k_pe_rot = rope(k_pe).astype(jnp.bfloat16)
    q_all = _bf16_dot(qn, wq_ref[...])
    kv_all = _bf16_dot(kvn, wkv_ref[...])
    scale = QK_HEAD_DIM ** -0.5 * _LOG2_E
    for hd in range(MLA_HEADS):
        b0 = hd * QK_PAD_DIM
        q_ref[:, b0:b0 + LANES] = (q_all[:, b0:b0 + LANES] * scale).astype(jnp.bfloat16)
        q_ref[:, b0 + LANES:b0 + QK_PAD_DIM] = (
            rope(q_all[:, b0 + LANES:b0 + QK_PAD_DIM]) * scale).astype(jnp.bfloat16)
        k_ref[:, b0:b0 + LANES] = kv_all[:, hd * LANES:(hd + 1) * LANES].astype(jnp.bfloat16)
        k_ref[:, b0 + LANES:b0 + QK_PAD_DIM] = k_pe_rot
    v_ref[...] = kv_all[:, MLA_WIDTH:].astype(jnp.bfloat16)


def _mixer_in(x2, pos2, mod3, g1, w_in_p, ln_g, ln_b, w_s, bs_t, q_g, kv_g, w_q_p, w_kv_p,
              ga, freqs, seq):
    m, d = x2.shape
    tm = MIX_TM
    per_batch = seq // tm
    row = lambda i: (i, 0)
    out_shapes = (
        jax.ShapeDtypeStruct((m, GMLP_WIDTH), jnp.bfloat16),
        jax.ShapeDtypeStruct((m, MLA_HEADS * QK_PAD_DIM), jnp.bfloat16),
        jax.ShapeDtypeStruct((m, MLA_HEADS * QK_PAD_DIM), jnp.bfloat16),
        jax.ShapeDtypeStruct((m, MLA_WIDTH), jnp.bfloat16),
    )
    return pl.pallas_call(
        _mixer_in_kernel,
        grid=(m // tm,),
        in_specs=[
            pl.BlockSpec((tm, d), row),
            pl.BlockSpec((tm, 1), row),
            pl.BlockSpec((1, N_MOD, d), lambda i: (i // per_batch, 0, 0)),
            _const_spec(g1.shape),
            _const_spec(w_in_p.shape),
            _const_spec(ln_g.shape),
            _const_spec(ln_b.shape),
            _const_spec(w_s.shape),
            _const_spec(bs_t.shape),
            _const_spec(q_g.shape),
            _const_spec(kv_g.shape),
            _const_spec(w_q_p.shape),
            _const_spec(w_kv_p.shape),
            _const_spec(ga.shape),
            _const_spec(freqs.shape),
        ],
        out_specs=[
            pl.BlockSpec((tm, GMLP_WIDTH), row),
            pl.BlockSpec((tm, MLA_HEADS * QK_PAD_DIM), row),
            pl.BlockSpec((tm, MLA_HEADS * QK_PAD_DIM), row),
            pl.BlockSpec((tm, MLA_WIDTH), row),
        ],
        out_shape=out_shapes,
        scratch_shapes=[pltpu.VMEM((tm, GMLP_WIDTH), jnp.float32)],
        compiler_params=pltpu.CompilerParams(
            dimension_semantics=("parallel",), vmem_limit_bytes=VMEM_LIMIT),
        name="mixer_in",
    )(x2, pos2, mod3, g1, w_in_p, ln_g, ln_b, w_s, bs_t, q_g, kv_g, w_q_p, w_kv_p, ga, freqs)


_NT_DIMS = (((1,), (1,)), ((), ()))


def _attention_kernel(q_ref, k_ref, v_ref, o_ref, s_scr, p_scr, a_scr, m_scr, l_scr, acc_scr):
    seq = q_ref.shape[0]
    tq, tk, rc = ATT_TQ, ATT_TK, ATT_ROWS
    n_sub = tq // tk
    assert n_sub % 2 == 0, "score buffers alternate per key sub-block"

    def scores(q0, q_rows, k0, slot):
        s_scr[slot, 0:q_rows, :] = lax.dot_general(
            q_ref[pl.ds(q0, q_rows), :], k_ref[pl.ds(k0, tk), :], _NT_DIMS,
            preferred_element_type=jnp.float32)

    def update(slot, row0, n_rows, v0, mask_col0):
        for r in range(0, n_rows, rc):
            rows = slice(row0 + r, row0 + r + rc)
            s = s_scr[slot, r:r + rc, :]
            if mask_col0 is not None and mask_col0 + tk > row0 + r + CHUNK:
                row_chunk = (row0 + r + lax.broadcasted_iota(jnp.int32, (rc, tk), 0)) // CHUNK
                col_chunk = (mask_col0 + lax.broadcasted_iota(jnp.int32, (rc, tk), 1)) // CHUNK
                s = jnp.where(col_chunk <= row_chunk, s, -jnp.inf)
            m_prev = m_scr[rows, :]
            m_new = jnp.maximum(m_prev, jnp.max(s, axis=-1, keepdims=True))
            alpha = jnp.exp2(m_prev - m_new)
            p = jnp.exp2(s - jnp.concatenate([m_new] * (tk // LANES), axis=1))
            p_lanes = p[:, 0:LANES]
            for c in range(LANES, tk, LANES):
                p_lanes = p_lanes + p[:, c:c + LANES]
            l_scr[rows, :] = alpha * l_scr[rows, :] + p_lanes
            m_scr[rows, :] = m_new
            p_scr[rows, :] = p.astype(jnp.bfloat16)
            a_scr[rows, :] = alpha
        pv_rows = ATT_PV_ROWS if n_rows % ATT_PV_ROWS == 0 else tk
        for r in range(row0, row0 + n_rows, pv_rows):
            rows = slice(r, r + pv_rows)
            pv = _bf16_dot(p_scr[rows, :], v_ref[pl.ds(v0, tk), :])
            acc_scr[rows, :] = a_scr[rows, :] * acc_scr[rows, :] + pv

    def q_body(qi, carry):
        q0 = pl.multiple_of(qi * tq, tq)
        m_scr[...] = jnp.full(m_scr.shape, -jnp.inf, jnp.float32)
        l_scr[...] = jnp.zeros(l_scr.shape, jnp.float32)
        acc_scr[...] = jnp.zeros(acc_scr.shape, jnp.float32)
        scores(q0, tq, 0, 0)

        def kv_body(jj, c):
            k0 = pl.multiple_of(jj * tq, tq)
            for t in range(n_sub):
                scores(q0, tq, k0 + (t + 1) * tk, (t + 1) % 2)
                update(t % 2, 0, tq, k0 + t * tk, None)
            return c

        lax.fori_loop(0, qi, kv_body, 0)
        for t in range(n_sub):
            if t + 1 < n_sub:
                scores(q0 + (t + 1) * tk, tq - (t + 1) * tk, q0 + (t + 1) * tk, (t + 1) % 2)
            update(t % 2, t * tk, tq - t * tk, q0 + t * tk, t * tk)
        l = jnp.sum(l_scr[...], axis=-1, keepdims=True)
        o_ref[pl.ds(q0, tq), :] = (acc_scr[...] / l).astype(o_ref.dtype)
        return carry

    lax.fori_loop(0, seq // tq, q_body, 0)


def _attention(q, k, v, batch, seq):
    m = q.shape[0]
    return pl.pallas_call(
        _attention_kernel,
        grid=(batch, MLA_HEADS),
        in_specs=[
            pl.BlockSpec((seq, QK_PAD_DIM), lambda b, h: (b, h)),
            pl.BlockSpec((seq, QK_PAD_DIM), lambda b, h: (b, h)),
            pl.BlockSpec((seq, V_HEAD_DIM), lambda b, h: (b, h)),
        ],
        out_specs=pl.BlockSpec((seq, V_HEAD_DIM), lambda b, h: (b, h)),
        out_shape=jax.ShapeDtypeStruct((m, MLA_WIDTH), jnp.bfloat16),
        scratch_shapes=[
            pltpu.VMEM((2, ATT_TQ, ATT_TK), jnp.float32),
            pltpu.VMEM((ATT_TQ, ATT_TK), jnp.bfloat16),
            pltpu.VMEM((ATT_TQ, LANES), jnp.float32),
            pltpu.VMEM((ATT_TQ, LANES), jnp.float32),
            pltpu.VMEM((ATT_TQ, LANES), jnp.float32),
            pltpu.VMEM((ATT_TQ, V_HEAD_DIM), jnp.float32),
        ],
        compiler_params=pltpu.CompilerParams(
            dimension_semantics=("parallel", "parallel"), vmem_limit_bytes=VMEM_LIMIT),
        name="attention",
    )(q, k, v)


def _mixer_out_kernel(x_ref, an_ref, m_ref, mod_ref, gb_ref, wo_ref, gp1_ref, g2_ref,
                      x1_ref, h2_ref):
    g1 = mod_ref[0, 2:3, :]
    sh2 = mod_ref[0, 3:4, :]
    sc2 = mod_ref[0, 4:5, :]
    mn = _rms(m_ref[...].astype(jnp.float32), gb_ref[...]).astype(jnp.bfloat16)
    y = _bf16_dot(an_ref[...], wo_ref[0:GMLP_WIDTH, :]) + _bf16_dot(mn, wo_ref[GMLP_WIDTH:, :])
    x1 = x_ref[...] + g1 * _rms(y, gp1_ref[...])
    x1_ref[...] = x1
    h2_ref[...] = (_rms(x1, g2_ref[...]) * (1.0 + sc2) + sh2).astype(jnp.bfloat16)


def _mixer_out(x2, an, mo, mod3, gb, w_o, gp1, g2, seq):
    m, d = x2.shape
    tm = OUT_TM
    per_batch = seq // tm
    row = lambda i: (i, 0)
    return pl.pallas_call(
        _mixer_out_kernel,
        grid=(m // tm,),
        in_specs=[
            pl.BlockSpec((tm, d), row),
            pl.BlockSpec((tm, GMLP_WIDTH), row),
            pl.BlockSpec((tm, MLA_WIDTH), row),
            pl.BlockSpec((1, N_MOD, d), lambda i: (i // per_batch, 0, 0)),
            _const_spec(gb.shape),
            _const_spec(w_o.shape),
            _const_spec(gp1.shape),
            _const_spec(g2.shape),
        ],
        out_specs=[pl.BlockSpec((tm, d), row), pl.BlockSpec((tm, d), row)],
        out_shape=(jax.ShapeDtypeStruct((m, d), jnp.float32),
                   jax.ShapeDtypeStruct((m, d), jnp.bfloat16)),
        compiler_params=pltpu.CompilerParams(
            dimension_semantics=("parallel",), vmem_limit_bytes=VMEM_LIMIT),
        name="mixer_out",
    )(x2, an, mo, mod3, gb, w_o, gp1, g2)


def _ffn_kernel(h_ref, x1_ref, mod_ref, w1_ref, w2_ref, gp2_ref, o_ref):
    j = pl.program_id(1)

    @pl.when(j == 0)
    def _():
        o_ref[...] = jnp.zeros(o_ref.shape, o_ref.dtype)

    t = jnp.maximum(_bf16_dot(h_ref[...], w1_ref[0]), 0.0)
    o_ref[...] += _bf16_dot((t * t).astype(jnp.bfloat16), w2_ref[...])

    @pl.when(j == pl.num_programs(1) - 1)
    def _():
        g2 = mod_ref[0, 5:6, :]
        o_ref[...] = x1_ref[...] + g2 * _rms(o_ref[...], gp2_ref[...])


def _ffn(h2, x1, mod3, w1, w2, gp2, seq):
    m, d = x1.shape
    tm, tf = FFN_TM, FFN_TF
    per_batch = seq // tm
    return pl.pallas_call(
        _ffn_kernel,
        grid=(m // tm, D_FF // tf),
        in_specs=[
            pl.BlockSpec((tm, d), lambda i, j: (i, 0)),
            pl.BlockSpec((tm, d), lambda i, j: (i, 0), pipeline_mode=pl.Buffered(1)),
            pl.BlockSpec((1, N_MOD, d), lambda i, j: (i // per_batch, 0, 0)),
            pl.BlockSpec((1, d, tf), lambda i, j: (j, 0, 0)),
            pl.BlockSpec((tf, d), lambda i, j: (j, 0)),
            _const_spec(gp2.shape),
        ],
        out_specs=pl.BlockSpec((tm, d), lambda i, j: (i, 0)),
        out_shape=jax.ShapeDtypeStruct((m, d), jnp.float32),
        compiler_params=pltpu.CompilerParams(
            dimension_semantics=("parallel", "arbitrary"), vmem_limit_bytes=VMEM_LIMIT),
        name="ffn",
    )(h2, x1, mod3, w1, w2, gp2)


def _rope_cols(w_pe):
    x1, x2 = w_pe[..., :ROPE_HALF], w_pe[..., ROPE_HALF:]
    return jnp.concatenate([x1, x2, -x2, x1], axis=-1)


def kernel(x, c, positions, w_ada, b_ada, pre_norm1_g, w_in, ln_v_g, ln_v_b, w_s, b_s, q_norm_g, w_q_b, kv_norm_g, w_kv_b, grp_norm_a_g, grp_norm_b_g, w_o, post_norm1_g, pre_norm2_g, w_ff1, w_ff2, post_norm2_g):
    batch, seq, d = x.shape
    depth = w_ada.shape[0]
    bf = jnp.bfloat16
    freqs = ROPE_THETA ** (-jnp.arange(0, QK_ROPE_DIM, 2, dtype=jnp.float32) / QK_ROPE_DIM)
    freqs = jnp.tile(freqs, LANES // ROPE_HALF).reshape(1, LANES)
    pos2 = positions.reshape(batch * seq, 1)
    x2 = x.reshape(batch * seq, d)
    o_kr = IN_LAT + Q_LORA_RANK + KV_LORA_RANK

    for l in range(depth):
        w_in_p = jnp.concatenate([w_in[l][:, :o_kr], _rope_cols(w_in[l][:, o_kr:])], axis=1).astype(bf)
        wq = w_q_b[l].reshape(Q_LORA_RANK, MLA_HEADS, QK_HEAD_DIM)
        w_q_p = jnp.concatenate([wq[..., :QK_NOPE_DIM], _rope_cols(wq[..., QK_NOPE_DIM:])], axis=-1)
        w_q_p = w_q_p.reshape(Q_LORA_RANK, MLA_HEADS * QK_PAD_DIM).astype(bf)
        wkv = w_kv_b[l].reshape(KV_LORA_RANK, MLA_HEADS, QK_NOPE_DIM + V_HEAD_DIM)
        w_kv_p = jnp.concatenate(
            [wkv[..., :QK_NOPE_DIM].reshape(KV_LORA_RANK, -1),
             wkv[..., QK_NOPE_DIM:].reshape(KV_LORA_RANK, -1)], axis=1).astype(bf)

        mod3 = _adaln(c, w_ada[l], b_ada[l]).reshape(batch, N_MOD, d)
        an, q, k, v = _mixer_in(
            x2, pos2, mod3, pre_norm1_g[l].reshape(1, d), w_in_p,
            ln_v_g[l].reshape(1, -1), ln_v_b[l].reshape(1, -1), w_s[l], b_s[l].T,
            q_norm_g[l].reshape(1, -1), kv_norm_g[l].reshape(1, -1), w_q_p, w_kv_p,
            grp_norm_a_g[l].reshape(1, -1), freqs, seq)
        mo = _attention(q, k, v, batch, seq)
        x1, h2 = _mixer_out(
            x2, an, mo, mod3, grp_norm_b_g[l].reshape(1, -1), w_o[l].astype(bf),
            post_norm1_g[l].reshape(1, d), pre_norm2_g[l].reshape(1, d), seq)
        w1_chunks = w_ff1[l].astype(bf).reshape(d, D_FF // FFN_TF, FFN_TF).transpose(1, 0, 2)
        x2 = _ffn(h2, x1, mod3, w1_chunks, w_ff2[l].astype(bf),
                  post_norm2_g[l].reshape(1, d), seq)
    return x2.reshape(batch, seq, d)
```

```python
import functools
import math

import jax
import jax.numpy as jnp
from jax import lax
from jax.experimental import pallas as pl
from jax.experimental.pallas import tpu as pltpu

D_MODEL = 2048
CHUNK = 64
GMLP_WIDTH = 1024
GMLP_GROUPS = 8
GMLP_GROUP_DIM = 128
GMLP_BLOCK = 128
MLA_WIDTH = 1024
MLA_HEADS = 8
QK_NOPE_DIM = 128
QK_ROPE_DIM = 64
V_HEAD_DIM = 128
Q_LORA_RANK = 512
KV_LORA_RANK = 256
QK_HEAD_DIM = QK_NOPE_DIM + QK_ROPE_DIM
D_FF = 4 * D_MODEL
ROPE_THETA = 10000.0
EPS = 1e-6
N_MOD = 6

LANES = 128
QK_PAD_DIM = 2 * LANES
ROPE_HALF = QK_ROPE_DIM // 2
IN_LAT = GMLP_WIDTH * 2
IN_PROJ_PAD = IN_LAT + Q_LORA_RANK + KV_LORA_RANK + LANES

VMEM_LIMIT = 56 * 1024 * 1024

ADA_TN = 1024
MIX_TM = 512
MIX_SUB = 512
ATT_TQ = 1024
ATT_TK = 256
ATT_ROWS = 128
ATT_PV_ROWS = 512
ATT_HEADS = 1
OUT_TM = 512
OUT_SUB = 256
FFN_TM = 1024
FFN_TF = 1024
EPI_ROWS = 64

_SQRT_HALF = math.sqrt(0.5)
_LOG2_E = math.log2(math.e)


def _rms(x, g):
    return x * lax.rsqrt(jnp.mean(x * x, axis=-1, keepdims=True) + EPS) * g


def _gelu(x):
    return 0.5 * x * (1.0 + lax.erf(x * _SQRT_HALF))


def _bf16_dot(a, b):
    return jnp.dot(a, b, preferred_element_type=jnp.float32)


def _const_spec(shape):
    nd = len(shape)
    return pl.BlockSpec(shape, lambda *_: (0,) * nd, pipeline_mode=pl.Buffered(1))


def _adaln_kernel(c_ref, w_ref, b_ref, o_ref):
    c = c_ref[...]
    cs = (c * jax.nn.sigmoid(c)).astype(jnp.bfloat16)
    o_ref[...] = _bf16_dot(cs, w_ref[...].astype(jnp.bfloat16)) + b_ref[...]


def _adaln(c, w_ada, b_ada):
    b, d = c.shape
    n = w_ada.shape[1]
    return pl.pallas_call(
        _adaln_kernel,
        grid=(n // ADA_TN,),
        in_specs=[
            pl.BlockSpec((b, d), lambda j: (0, 0)),
            pl.BlockSpec((d, ADA_TN), lambda j: (0, j)),
            pl.BlockSpec((1, ADA_TN), lambda j: (0, j)),
        ],
        out_specs=pl.BlockSpec((b, ADA_TN), lambda j: (0, j)),
        out_shape=jax.ShapeDtypeStruct((b, n), jnp.float32),
        compiler_params=pltpu.CompilerParams(
            dimension_semantics=("parallel",), vmem_limit_bytes=VMEM_LIMIT),
        name="adaln",
    )(c, w_ada, b_ada.reshape(1, n))


def _mixer_in_kernel(x_ref, pos_ref, mod_ref, g1_ref, w_in_ref, w_kr_ref, lng_ref, lnb_ref,
                     ws_ref, bs_ref, qg_ref, kvg_ref, wq_ref, wkv_ref, ga_ref, fr_ref,
                     an_ref, q_ref, k_ref, v_ref, a_scr):
    sh1 = mod_ref[0, 0:1, :]
    sc1 = mod_ref[0, 1:2, :]
    row_chunk = lax.broadcasted_iota(jnp.int32, (GMLP_BLOCK, GMLP_BLOCK), 0) // CHUNK
    col_chunk = lax.broadcasted_iota(jnp.int32, (GMLP_BLOCK, GMLP_BLOCK), 1) // CHUNK
    visible = col_chunk <= row_chunk
    w_spatial = [jnp.where(visible, ws_ref[g], 0.0).astype(jnp.bfloat16)
                 for g in range(GMLP_GROUPS)]
    o_q = IN_LAT
    o_kv = o_q + Q_LORA_RANK
    o_kr = o_kv + KV_LORA_RANK
    scale = QK_HEAD_DIM ** -0.5 * _LOG2_E
    n_blk = MIX_SUB // GMLP_BLOCK

    for r0 in range(0, x_ref.shape[0], MIX_SUB):
        rs = slice(r0, r0 + MIX_SUB)
        h = (_rms(x_ref[rs, :], g1_ref[...]) * (1.0 + sc1) + sh1).astype(jnp.bfloat16)

        ug = _gelu(_bf16_dot(h, w_in_ref[:, 0:GMLP_WIDTH]))
        vg = _gelu(_bf16_dot(h, w_in_ref[:, GMLP_WIDTH:IN_LAT]))
        mu = jnp.mean(vg, axis=-1, keepdims=True)
        vc = vg - mu
        var = jnp.mean(vc * vc, axis=-1, keepdims=True)
        vn = (vc * lax.rsqrt(var + EPS) * lng_ref[...] + lnb_ref[...]).astype(jnp.bfloat16)
        for g in range(GMLP_GROUPS):
            c0 = g * GMLP_GROUP_DIM
            v_g = jnp.concatenate(
                [vn[b * GMLP_BLOCK:(b + 1) * GMLP_BLOCK, c0:c0 + GMLP_GROUP_DIM] for b in range(n_blk)],
                axis=1)
            s_g = _bf16_dot(w_spatial[g], v_g)
            bias = bs_ref[:, g:g + 1]
            for b in range(n_blk):
                rows = slice(b * GMLP_BLOCK, (b + 1) * GMLP_BLOCK)
                s_bg = s_g[:, b * GMLP_GROUP_DIM:(b + 1) * GMLP_GROUP_DIM] + bias
                a_scr[r0 + b * GMLP_BLOCK:r0 + (b + 1) * GMLP_BLOCK, c0:c0 + GMLP_GROUP_DIM] = (
                    ug[rows, c0:c0 + GMLP_GROUP_DIM] * s_bg)
        an_ref[rs, :] = _rms(a_scr[rs, :], ga_ref[...]).astype(jnp.bfloat16)

        qn = _rms(_bf16_dot(h, w_in_ref[:, o_q:o_kv]), qg_ref[...]).astype(jnp.bfloat16)
        kvn = _rms(_bf16_dot(h, w_in_ref[:, o_kv:o_kr]), kvg_ref[...]).astype(jnp.bfloat16)
        k_pe = _bf16_dot(h, w_kr_ref[...])

        ang = pos_ref[rs, :].astype(jnp.float32) * fr_ref[...]
        lane = lax.broadcasted_iota(jnp.int32, (MIX_SUB, LANES), 1)
        in_rope = lane < QK_ROPE_DIM
        cos_t = jnp.where(in_rope, jnp.cos(ang), 0.0)
        sin_t = jnp.where(in_rope, jnp.sin(ang), 0.0)

        def rope(xcol):
            return xcol * cos_t + pltpu.roll(xcol, 2 * ROPE_HALF, 1) * sin_t

        k_pe_rot = rope(k_pe).astype(jnp.bfloat16)
        q_all = _bf16_dot(qn, wq_ref[...])
        kv_all = _bf16_dot(kvn, wkv_ref[...])
        for hd in range(MLA_HEADS):
            b0 = hd * QK_PAD_DIM
            q_ref[rs, b0:b0 + LANES] = (q_all[:, b0:b0 + LANES] * scale).astype(jnp.bfloat16)
            q_ref[rs, b0 + LANES:b0 + QK_PAD_DIM] = (
                rope(q_all[:, b0 + LANES:b0 + QK_PAD_DIM]) * scale).astype(jnp.bfloat16)
            k_ref[rs, b0:b0 + LANES] = kv_all[:, hd * LANES:(hd + 1) * LANES].astype(jnp.bfloat16)
            k_ref[rs, b0 + LANES:b0 + QK_PAD_DIM] = k_pe_rot
        v_ref[rs, :] = kv_all[:, MLA_WIDTH:].astype(jnp.bfloat16)


def _mixer_in(x2, pos2, mod3, g1, w_in_b, w_kr, ln_g, ln_b, w_s, bs_t, q_g, kv_g, w_q_p, w_kv_p,
              ga, freqs, seq):
    m, d = x2.shape
    tm = MIX_TM
    per_batch = seq // tm
    row = lambda i: (i, 0)
    out_shapes = (
        jax.ShapeDtypeStruct((m, GMLP_WIDTH), jnp.bfloat16),
        jax.ShapeDtypeStruct((m, MLA_HEADS * QK_PAD_DIM), jnp.bfloat16),
        jax.ShapeDtypeStruct((m, MLA_HEADS * QK_PAD_DIM), jnp.bfloat16),
        jax.ShapeDtypeStruct((m, MLA_WIDTH), jnp.bfloat16),
    )
    return pl.pallas_call(
        _mixer_in_kernel,
        grid=(m // tm,),
        in_specs=[
            pl.BlockSpec((tm, d), row),
            pl.BlockSpec((tm, 1), row),
            pl.BlockSpec((1, N_MOD, d), lambda i: (i // per_batch, 0, 0)),
            _const_spec(g1.shape),
            _const_spec(w_in_b.shape),
            _const_spec(w_kr.shape),
            _const_spec(ln_g.shape),
            _const_spec(ln_b.shape),
            _const_spec(w_s.shape),
            _const_spec(bs_t.shape),
            _const_spec(q_g.shape),
            _const_spec(kv_g.shape),
            _const_spec(w_q_p.shape),
            _const_spec(w_kv_p.shape),
            _const_spec(ga.shape),
            _const_spec(freqs.shape),
        ],
        out_specs=[
            pl.BlockSpec((tm, GMLP_WIDTH), row),
            pl.BlockSpec((tm, MLA_HEADS * QK_PAD_DIM), row),
            pl.BlockSpec((tm, MLA_HEADS * QK_PAD_DIM), row),
            pl.BlockSpec((tm, MLA_WIDTH), row),
        ],
        out_shape=out_shapes,
        scratch_shapes=[pltpu.VMEM((tm, GMLP_WIDTH), jnp.float32)],
        compiler_params=pltpu.CompilerParams(
            dimension_semantics=("parallel",), vmem_limit_bytes=VMEM_LIMIT),
        name="mixer_in",
    )(x2, pos2, mod3, g1, w_in_b, w_kr, ln_g, ln_b, w_s, bs_t, q_g, kv_g, w_q_p, w_kv_p, ga, freqs)


_NT_DIMS = (((1,), (1,)), ((), ()))


def _attention_kernel(q_ref, k_ref, v_ref, o_ref, s_scr, p_scr, a_scr, m_scr, l_scr, acc_scr):
    seq = q_ref.shape[0]
    tq, tk, rc = ATT_TQ, ATT_TK, ATT_ROWS
    n_sub = tq // tk
    assert n_sub % 2 == 0, "score buffers alternate per key sub-block"
    heads = range(ATT_HEADS)

    def scores(hd, q0, q_rows, k0, slot):
        cols = slice(hd * QK_PAD_DIM, (hd + 1) * QK_PAD_DIM)
        s_scr[hd, slot, 0:q_rows, :] = lax.dot_general(
            q_ref[pl.ds(q0, q_rows), cols], k_ref[pl.ds(k0, tk), cols], _NT_DIMS,
            preferred_element_type=jnp.float32)

    def update(hd, slot, row0, n_rows, v0, mask_col0):
        for r in range(0, n_rows, rc):
            rows = slice(row0 + r, row0 + r + rc)
            s = s_scr[hd, slot, r:r + rc, :]
            if mask_col0 is not None and mask_col0 + tk > row0 + r + CHUNK:
                row_chunk = (row0 + r + lax.broadcasted_iota(jnp.int32, (rc, tk), 0)) // CHUNK
                col_chunk = (mask_col0 + lax.broadcasted_iota(jnp.int32, (rc, tk), 1)) // CHUNK
                s = jnp.where(col_chunk <= row_chunk, s, -jnp.inf)
            m_prev = m_scr[hd, rows, :]
            m_new = jnp.maximum(m_prev, jnp.max(s, axis=-1, keepdims=True))
            alpha = jnp.exp2(m_prev - m_new)
            p = jnp.exp2(s - jnp.concatenate([m_new] * (tk // LANES), axis=1))
            p_lanes = p[:, 0:LANES]
            for c in range(LANES, tk, LANES):
                p_lanes = p_lanes + p[:, c:c + LANES]
            l_scr[hd, rows, :] = alpha * l_scr[hd, rows, :] + p_lanes
            m_scr[hd, rows, :] = m_new
            p_scr[hd, rows, :] = p.astype(jnp.bfloat16)
            a_scr[hd, rows, :] = alpha
        pv_rows = ATT_PV_ROWS if n_rows % ATT_PV_ROWS == 0 else tk
        v_cols = slice(hd * V_HEAD_DIM, (hd + 1) * V_HEAD_DIM)
        for r in range(row0, row0 + n_rows, pv_rows):
            rows = slice(r, r + pv_rows)
            pv = _bf16_dot(p_scr[hd, rows, :], v_ref[pl.ds(v0, tk), v_cols])
            acc_scr[hd, rows, :] = a_scr[hd, rows, :] * acc_scr[hd, rows, :] + pv

    def q_body(qi, carry):
        q0 = pl.multiple_of(qi * tq, tq)
        m_scr[...] = jnp.full(m_scr.shape, -jnp.inf, jnp.float32)
        l_scr[...] = jnp.zeros(l_scr.shape, jnp.float32)
        acc_scr[...] = jnp.zeros(acc_scr.shape, jnp.float32)
        for hd in heads:
            scores(hd, q0, tq, 0, 0)

        def kv_body(jj, c):
            k0 = pl.multiple_of(jj * tq, tq)
            for t in range(n_sub):
                for hd in heads:
                    scores(hd, q0, tq, k0 + (t + 1) * tk, (t + 1) % 2)
                for hd in heads:
                    update(hd, t % 2, 0, tq, k0 + t * tk, None)
            return c

        lax.fori_loop(0, qi, kv_body, 0)
        for t in range(n_sub):
            if t + 1 < n_sub:
                for hd in heads:
                    scores(hd, q0 + (t + 1) * tk, tq - (t + 1) * tk, q0 + (t + 1) * tk, (t + 1) % 2)
            for hd in heads:
                update(hd, t % 2, t * tk, tq - t * tk, q0 + t * tk, t * tk)
        for hd in heads:
            l = jnp.sum(l_scr[hd], axis=-1, keepdims=True)
            o_ref[pl.ds(q0, tq), hd * V_HEAD_DIM:(hd + 1) * V_HEAD_DIM] = (
                acc_scr[hd] / l).astype(o_ref.dtype)
        return carry

    lax.fori_loop(0, seq // tq, q_body, 0)


def _attention(q, k, v, batch, seq):
    m = q.shape[0]
    nh = ATT_HEADS
    return pl.pallas_call(
        _attention_kernel,
        grid=(batch, MLA_HEADS // nh),
        in_specs=[
            pl.BlockSpec((seq, nh * QK_PAD_DIM), lambda b, h: (b, h)),
            pl.BlockSpec((seq, nh * QK_PAD_DIM), lambda b, h: (b, h)),
            pl.BlockSpec((seq, nh * V_HEAD_DIM), lambda b, h: (b, h)),
        ],
        out_specs=pl.BlockSpec((seq, nh * V_HEAD_DIM), lambda b, h: (b, h)),
        out_shape=jax.ShapeDtypeStruct((m, MLA_WIDTH), jnp.bfloat16),
        scratch_shapes=[
            pltpu.VMEM((nh, 2, ATT_TQ, ATT_TK), jnp.float32),
            pltpu.VMEM((nh, ATT_TQ, ATT_TK), jnp.bfloat16),
            pltpu.VMEM((nh, ATT_TQ, LANES), jnp.float32),
            pltpu.VMEM((nh, ATT_TQ, LANES), jnp.float32),
            pltpu.VMEM((nh, ATT_TQ, LANES), jnp.float32),
            pltpu.VMEM((nh, ATT_TQ, V_HEAD_DIM), jnp.float32),
        ],
        compiler_params=pltpu.CompilerParams(
            dimension_semantics=("parallel", "parallel"), vmem_limit_bytes=VMEM_LIMIT),
        name="attention",
    )(q, k, v)


def _mixer_out_kernel(x_ref, an_ref, m_ref, mod_ref, gb_ref, wo_ref, gp1_ref, g2_ref,
                      x1_ref, h2_ref):
    g1 = mod_ref[0, 2:3, :]
    sh2 = mod_ref[0, 3:4, :]
    sc2 = mod_ref[0, 4:5, :]
    for r in range(0, x_ref.shape[0], OUT_SUB):
        rows = slice(r, r + OUT_SUB)
        mn = _rms(m_ref[rows, :].astype(jnp.float32), gb_ref[...]).astype(jnp.bfloat16)
        y = (_bf16_dot(an_ref[rows, :], wo_ref[0:GMLP_WIDTH, :])
             + _bf16_dot(mn, wo_ref[GMLP_WIDTH:, :]))
        x1 = x_ref[rows, :] + g1 * _rms(y, gp1_ref[...])
        x1_ref[rows, :] = x1
        h2_ref[rows, :] = (_rms(x1, g2_ref[...]) * (1.0 + sc2) + sh2).astype(jnp.bfloat16)


def _mixer_out(x2, an, mo, mod3, gb, w_o, gp1, g2, seq):
    m, d = x2.shape
    tm = OUT_TM
    per_batch = seq // tm
    row = lambda i: (i, 0)
    return pl.pallas_call(
        _mixer_out_kernel,
        grid=(m // tm,),
        in_specs=[
            pl.BlockSpec((tm, d), row),
            pl.BlockSpec((tm, GMLP_WIDTH), row),
            pl.BlockSpec((tm, MLA_WIDTH), row),
            pl.BlockSpec((1, N_MOD, d), lambda i: (i // per_batch, 0, 0)),
            _const_spec(gb.shape),
            _const_spec(w_o.shape),
            _const_spec(gp1.shape),
            _const_spec(g2.shape),
        ],
        out_specs=[pl.BlockSpec((tm, d), row), pl.BlockSpec((tm, d), row)],
        out_shape=(jax.ShapeDtypeStruct((m, d), jnp.float32),
                   jax.ShapeDtypeStruct((m, d), jnp.bfloat16)),
        compiler_params=pltpu.CompilerParams(
            dimension_semantics=("parallel",), vmem_limit_bytes=VMEM_LIMIT),
        name="mixer_out",
    )(x2, an, mo, mod3, gb, w_o, gp1, g2)


def _ffn_kernel(h_ref, x1_hbm, mod_ref, w1_ref, w2_ref, gp2_ref, o_ref, x1_buf, x1_sem):
    i = pl.program_id(0)
    j = pl.program_id(1)
    tm = o_ref.shape[0]
    x1_copy = pltpu.make_async_copy(
        x1_hbm.at[pl.ds(pl.multiple_of(i * tm, tm), tm), :], x1_buf, x1_sem)

    @pl.when(j == 0)
    def _():
        x1_copy.start()
        o_ref[...] = jnp.zeros(o_ref.shape, o_ref.dtype)

    t = jnp.maximum(_bf16_dot(h_ref[...], w1_ref[...]), 0.0)
    o_ref[...] += _bf16_dot((t * t).astype(jnp.bfloat16), w2_ref[...])

    @pl.when(j == pl.num_programs(1) - 1)
    def _():
        x1_copy.wait()
        g2 = mod_ref[0, 5:6, :]
        for r in range(0, tm, EPI_ROWS):
            rows = slice(r, r + EPI_ROWS)
            o_ref[rows, :] = x1_buf[rows, :] + g2 * _rms(o_ref[rows, :], gp2_ref[...])


def _ffn(h2, x1, mod3, w1, w2, gp2, seq):
    m, d = x1.shape
    tm, tf = FFN_TM, FFN_TF
    per_batch = seq // tm
    return pl.pallas_call(
        _ffn_kernel,
        grid=(m // tm, D_FF // tf),
        in_specs=[
            pl.BlockSpec((tm, d), lambda i, j: (i, 0)),
            pl.BlockSpec(memory_space=pltpu.HBM),
            pl.BlockSpec((1, N_MOD, d), lambda i, j: (i // per_batch, 0, 0)),
            pl.BlockSpec((d, tf), lambda i, j: (0, j)),
            pl.BlockSpec((tf, d), lambda i, j: (j, 0)),
            _const_spec(gp2.shape),
        ],
        out_specs=pl.BlockSpec((tm, d), lambda i, j: (i, 0)),
        out_shape=jax.ShapeDtypeStruct((m, d), jnp.float32),
        scratch_shapes=[pltpu.VMEM((tm, d), jnp.float32), pltpu.SemaphoreType.DMA(())],
        compiler_params=pltpu.CompilerParams(
            dimension_semantics=("parallel", "arbitrary"), vmem_limit_bytes=VMEM_LIMIT),
        name="ffn",
    )(h2, x1, mod3, w1, w2, gp2)


def _rope_cols(w_pe):
    x1, x2 = w_pe[..., :ROPE_HALF], w_pe[..., ROPE_HALF:]
    return jnp.concatenate([x1, x2, -x2, x1], axis=-1)


def kernel(x, c, positions, w_ada, b_ada, pre_norm1_g, w_in, ln_v_g, ln_v_b, w_s, b_s, q_norm_g, w_q_b, kv_norm_g, w_kv_b, grp_norm_a_g, grp_norm_b_g, w_o, post_norm1_g, pre_norm2_g, w_ff1, w_ff2, post_norm2_g):
    batch, seq, d = x.shape
    depth = w_ada.shape[0]
    bf = jnp.bfloat16
    freqs = ROPE_THETA ** (-jnp.arange(0, QK_ROPE_DIM, 2, dtype=jnp.float32) / QK_ROPE_DIM)
    freqs = jnp.tile(freqs, LANES // ROPE_HALF).reshape(1, LANES)
    pos2 = positions.reshape(batch * seq, 1)
    x2 = x.reshape(batch * seq, d)
    o_kr = IN_LAT + Q_LORA_RANK + KV_LORA_RANK

    for l in range(depth):
        w_in_b = w_in[l].astype(bf)
        w_kr = _rope_cols(w_in[l][:, o_kr:]).astype(bf)
        wq = w_q_b[l].reshape(Q_LORA_RANK, MLA_HEADS, QK_HEAD_DIM)
        w_q_p = jnp.concatenate([wq[..., :QK_NOPE_DIM], _rope_cols(wq[..., QK_NOPE_DIM:])], axis=-1)
        w_q_p = w_q_p.reshape(Q_LORA_RANK, MLA_HEADS * QK_PAD_DIM).astype(bf)
        wkv = w_kv_b[l].reshape(KV_LORA_RANK, MLA_HEADS, QK_NOPE_DIM + V_HEAD_DIM)
        w_kv_p = jnp.concatenate(
            [wkv[..., :QK_NOPE_DIM].reshape(KV_LORA_RANK, -1),
             wkv[..., QK_NOPE_DIM:].reshape(KV_LORA_RANK, -1)], axis=1).astype(bf)

        mod3 = _adaln(c, w_ada[l], b_ada[l]).reshape(batch, N_MOD, d)
        an, q, k, v = _mixer_in(
            x2, pos2, mod3, pre_norm1_g[l].reshape(1, d), w_in_b, w_kr,
            ln_v_g[l].reshape(1, -1), ln_v_b[l].reshape(1, -1), w_s[l], b_s[l].T,
            q_norm_g[l].reshape(1, -1), kv_norm_g[l].reshape(1, -1), w_q_p, w_kv_p,
            grp_norm_a_g[l].reshape(1, -1), freqs, seq)
        mo = _attention(q, k, v, batch, seq)
        x1, h2 = _mixer_out(
            x2, an, mo, mod3, grp_norm_b_g[l].reshape(1, -1), w_o[l].astype(bf),
            post_norm1_g[l].reshape(1, d), pre_norm2_g[l].reshape(1, d), seq)
        x2 = _ffn(h2, x1, mod3, w_ff1[l].astype(bf), w_ff2[l].astype(bf),
                  post_norm2_g[l].reshape(1, d), seq)
    return x2.reshape(batch, seq, d)
```
